```python
import jax, jax.numpy as jnp
from jax import lax
import numpy as np

D_MODEL = 4096
BATCH = 1
SEQ = 8192
DEPTH = 1

CHUNK = 64
Q_BLOCK = 128

D_MIX = D_MODEL
GLA_WIDTH = D_MIX // 2
GLA_HEADS = 16
GLA_DV = GLA_WIDTH // GLA_HEADS
GLA_DK = GLA_DV // 2
GLA_GATE_RANK = 16
GLA_GATE_TAU = 16.0

MLA_WIDTH = D_MIX - GLA_WIDTH
MLA_HEADS = 16
MLA_DV = MLA_WIDTH // MLA_HEADS
MLA_DN = 128
MLA_DR = 64
MLA_Q_RANK = 1536
MLA_KV_RANK = 512
ROPE_THETA = 10000.0

IN_SPLITS = (
    GLA_HEADS * GLA_DK,
    GLA_HEADS * GLA_DK,
    GLA_WIDTH,
    GLA_GATE_RANK,
    GLA_WIDTH,
    MLA_Q_RANK,
    MLA_KV_RANK,
    MLA_DR,
    MLA_WIDTH,
)
IN_WIDTH = sum(IN_SPLITS)

EPS = 1e-6

kernel_name = "hybrid_gla_mla_adaln_sandwich"


def rms_norm(t, g):
    tf = t.astype(jnp.float32)
    y = tf * lax.rsqrt(jnp.mean(tf * tf, axis=-1, keepdims=True) + EPS)
    return (y * g.astype(jnp.float32)).astype(t.dtype)


def apply_rope(t, cos, sin):
    tf = t.astype(jnp.float32)
    t1, t2 = jnp.split(tf, 2, axis=-1)
    return jnp.concatenate([t1 * cos - t2 * sin, t2 * cos + t1 * sin], axis=-1).astype(t.dtype)


def gla_chunk_causal(q, k, v, log_a):
    B, S, H, DK = q.shape
    DV = v.shape[-1]
    N = S // CHUNK
    f32 = jnp.float32
    qc = q.reshape(B, N, CHUNK, H, DK).astype(f32)
    kc = k.reshape(B, N, CHUNK, H, DK).astype(f32)
    vc = v.reshape(B, N, CHUNK, H, DV).astype(f32)
    la = log_a.reshape(B, N, CHUNK, H, DK).astype(f32)
    lcum = jnp.cumsum(la, axis=2)
    ltot = lcum[:, :, -1]
    k_dec = kc * jnp.exp(ltot[:, :, None] - lcum)
    q_dec = qc * jnp.exp(ltot)[:, :, None]
    scores = jnp.einsum('bnchd,bnshd->bnhcs', qc, k_dec)
    o_intra = jnp.einsum('bnhcs,bnshv->bnchv', scores, vc)
    kv = jnp.einsum('bnshd,bnshv->bnhdv', k_dec, vc)

    def step(state, inp):
        decay, kv_n = inp
        return decay[..., None] * state + kv_n, state

    s0 = jnp.zeros((B, H, DK, DV), f32)
    _, s_before = lax.scan(step, s0, (jnp.exp(ltot).swapaxes(0, 1), kv.swapaxes(0, 1)))
    s_before = s_before.swapaxes(0, 1)
    o_inter = jnp.einsum('bnchd,bnhdv->bnchv', q_dec, s_before)
    return (o_intra + o_inter).reshape(B, S, H, DV).astype(v.dtype)


def mla_chunk_causal(q_nope, q_rope, k_nope, k_rope, v):
    B, S, H, DN = q_nope.shape
    DR = q_rope.shape[-1]
    nb = S // Q_BLOCK
    scale = (DN + DR) ** -0.5
    key_chunk = jnp.arange(S) // CHUNK

    def block(args):
        qn, qr, i = args
        s = (jnp.einsum('bqhd,bkhd->bhqk', qn, k_nope)
             + jnp.einsum('bqhr,bkr->bhqk', qr, k_rope)).astype(jnp.float32) * scale
        q_chunk = (i * Q_BLOCK + jnp.arange(Q_BLOCK)) // CHUNK
        mask = key_chunk[None, :] <= q_chunk[:, None]
        s = jnp.where(mask[None, None], s, -jnp.inf)
        p = jax.nn.softmax(s, axis=-1).astype(v.dtype)
        return jnp.einsum('bhqk,bkhv->bqhv', p, v)

    qn_b = q_nope.reshape(B, nb, Q_BLOCK, H, DN).swapaxes(0, 1)
    qr_b = q_rope.reshape(B, nb, Q_BLOCK, H, DR).swapaxes(0, 1)
    out = lax.map(block, (qn_b, qr_b, jnp.arange(nb)))
    return out.swapaxes(0, 1).reshape(B, S, H, v.shape[-1])


def setup_inputs(seed: int = 0) -> dict:
    key = jax.random.key(seed)
    ks = jax.random.split(key, 20)
    f32 = jnp.float32
    nrm = lambda k, shape, s: jax.random.normal(k, shape, f32) * s
    x = nrm(ks[0], (BATCH, SEQ, D_MODEL), 1.0)
    c = nrm(ks[1], (BATCH, D_MODEL), 1.0)
    offset = jax.random.randint(ks[2], (BATCH, 1), 0, 4096, dtype=jnp.int32)
    positions = (offset + jnp.arange(SEQ, dtype=jnp.int32)[None, :]).astype(jnp.int32)
    w_ada = nrm(ks[3], (D_MODEL, 3 * D_MODEL), 0.5 * D_MODEL ** -0.5)
    b_ada = nrm(ks[4], (3 * D_MODEL,), 0.01)
    g_pre = 1.0 + nrm(ks[5], (D_MODEL,), 0.02)
    g_post = 1.0 + nrm(ks[6], (D_MODEL,), 0.02)
    w_in = nrm(ks[7], (D_MODEL, IN_WIDTH), D_MODEL ** -0.5)
    w_alpha_up = nrm(ks[8], (GLA_GATE_RANK, GLA_HEADS * GLA_DK), GLA_GATE_RANK ** -0.5)
    b_alpha = nrm(ks[9], (GLA_HEADS * GLA_DK,), 0.1)
    g_gla_out = 1.0 + nrm(ks[10], (GLA_DV,), 0.02)
    g_q_norm = 1.0 + nrm(ks[11], (MLA_Q_RANK,), 0.02)
    w_uq = nrm(ks[12], (MLA_Q_RANK, MLA_HEADS * (MLA_DN + MLA_DR)), MLA_Q_RANK ** -0.5)
    g_kv_norm = 1.0 + nrm(ks[13], (MLA_KV_RANK,), 0.02)
    w_ukv = nrm(ks[14], (MLA_KV_RANK, MLA_HEADS * (MLA_DN + MLA_DV)), MLA_KV_RANK ** -0.5)
    w_out = nrm(ks[15], (D_MIX, D_MODEL), D_MIX ** -0.5)
    return {"x": x, "c": c, "positions": positions, "w_ada": w_ada, "b_ada": b_ada,
            "g_pre": g_pre, "g_post": g_post, "w_in": w_in, "w_alpha_up": w_alpha_up,
            "b_alpha": b_alpha, "g_gla_out": g_gla_out, "g_q_norm": g_q_norm, "w_uq": w_uq,
            "g_kv_norm": g_kv_norm, "w_ukv": w_ukv, "w_out": w_out}


def reference(x, c, positions, w_ada, b_ada, g_pre, g_post, w_in, w_alpha_up, b_alpha,
              g_gla_out, g_q_norm, w_uq, g_kv_norm, w_ukv, w_out):
    B, S, D = x.shape
    half = MLA_DR // 2
    inv_freq = ROPE_THETA ** (-jnp.arange(half, dtype=jnp.float32) / half)
    ang = positions.astype(jnp.float32)[..., None] * inv_freq
    cos, sin = jnp.cos(ang), jnp.sin(ang)
    offsets = [int(o) for o in np.cumsum(IN_SPLITS)[:-1]]

    for _ in range(DEPTH):
        mod = jax.nn.silu(c) @ w_ada + b_ada
        shift, scale, gate = jnp.split(mod, 3, axis=-1)
        h = rms_norm(x, g_pre) * (1.0 + scale[:, None]) + shift[:, None]

        proj = h @ w_in
        (g_q, g_k, g_v, g_alr, g_gate,
         m_cq, m_ckv, m_kr, m_gate) = jnp.split(proj, offsets, axis=-1)

        q_a = g_q.reshape(B, S, GLA_HEADS, GLA_DK) * (GLA_DK ** -0.5)
        k_a = g_k.reshape(B, S, GLA_HEADS, GLA_DK)
        v_a = g_v.reshape(B, S, GLA_HEADS, GLA_DV)
        log_a = jax.nn.log_sigmoid((g_alr @ w_alpha_up + b_alpha).astype(jnp.float32)) / GLA_GATE_TAU
        log_a = log_a.reshape(B, S, GLA_HEADS, GLA_DK)
        o_a = gla_chunk_causal(q_a, k_a, v_a, log_a)
        o_a = rms_norm(o_a, g_gla_out).reshape(B, S, GLA_WIDTH) * jax.nn.silu(g_gate)

        q_b = (rms_norm(m_cq, g_q_norm) @ w_uq).reshape(B, S, MLA_HEADS, MLA_DN + MLA_DR)
        q_nope, q_rope = q_b[..., :MLA_DN], q_b[..., MLA_DN:]
        q_rope = apply_rope(q_rope, cos[:, :, None, :], sin[:, :, None, :])
        kv_b = (rms_norm(m_ckv, g_kv_norm) @ w_ukv).reshape(B, S, MLA_HEADS, MLA_DN + MLA_DV)
        k_nope, v_b = kv_b[..., :MLA_DN], kv_b[..., MLA_DN:]
        k_rope = apply_rope(m_kr, cos, sin)
        o_b = mla_chunk_causal(q_nope, q_rope, k_nope, k_rope, v_b)
        o_b = o_b.reshape(B, S, MLA_WIDTH) * jax.nn.silu(m_gate)

        mix = jnp.concatenate([o_a, o_b], axis=-1) @ w_out
        x = x + gate[:, None] * rms_norm(mix, g_post)
    return x
```

```python
import functools
import math

import jax
import jax.numpy as jnp
from jax import lax
from jax.experimental import pallas as pl
from jax.experimental.pallas import tpu as pltpu

CHUNK = 64

GLA_HEADS = 16
GLA_DK = 64
GLA_DV = 128
GLA_WIDTH = GLA_HEADS * GLA_DV
GLA_GATE_RANK = 16
GLA_GATE_TAU = 16.0

MLA_HEADS = 16
MLA_DN = 128
MLA_DR = 64
MLA_DV = 128
MLA_WIDTH = MLA_HEADS * MLA_DV
MLA_Q_RANK = 1536
MLA_KV_RANK = 512
MLA_DQK = 256
ROPE_THETA = 10000.0
EPS = 1e-6

LANES = 128
V7X_SCOPED_VMEM_BYTES = 60000 * 1024

F32 = jnp.float32
BF16 = jnp.bfloat16


def _nbytes(shape, dtype):
    return math.prod(shape) * jnp.dtype(dtype).itemsize


def _params(n_axes, blocks, temps=()):
    need = 2 * sum(_nbytes(s, d) for s, d in blocks) + sum(_nbytes(s, d) for s, d in temps)
    need += 2 * 1024 * 1024
    return pltpu.CompilerParams(
        dimension_semantics=("arbitrary",) * n_axes,
        vmem_limit_bytes=min(need, V7X_SCOPED_VMEM_BYTES),
    )


def _silu(t):
    return t * jax.nn.sigmoid(t)


def _rms_scale(t, g):
    ms = jnp.mean(t * t, axis=-1, keepdims=True)
    return t * lax.rsqrt(ms + EPS) * g


_ADALN_ROWS = 256


def _adaln_kernel(c_ref, w_ref, b_ref, o_ref):
    d = w_ref.shape[0]
    tn = w_ref.shape[1]

    def body(r, acc):
        rows = pl.ds(pl.multiple_of(r * _ADALN_ROWS, _ADALN_ROWS), _ADALN_ROWS)
        prod = w_ref[rows, :] * _silu(c_ref[rows, :])
        return acc + jnp.sum(prod.reshape(_ADALN_ROWS // 8, 8, tn), axis=0)

    acc = lax.fori_loop(0, d // _ADALN_ROWS, body, jnp.zeros((8, tn), F32))
    o_ref[...] = jnp.sum(acc, axis=0, keepdims=True) + b_ref[...]


def _adaln(c_col, w_ada, b_row, tn=512):
    d, n = w_ada.shape
    return pl.pallas_call(
        _adaln_kernel,
        grid=(n // tn,),
        in_specs=[
            pl.BlockSpec((d, 1), lambda j: (0, 0)),
            pl.BlockSpec((d, tn), lambda j: (0, j)),
            pl.BlockSpec((1, tn), lambda j: (0, j)),
        ],
        out_specs=pl.BlockSpec((1, tn), lambda j: (0, j)),
        out_shape=jax.ShapeDtypeStruct((1, n), F32),
        compiler_params=_params(1, [((d, LANES), F32), ((d, tn), F32)]),
        name="adaln",
    )(c_col, w_ada, b_row)


def _prenorm_kernel(x_ref, g_ref, shift_ref, scale_ref, h_ref):
    y = _rms_scale(x_ref[...], g_ref[...])
    h_ref[...] = (y * (1.0 + scale_ref[...]) + shift_ref[...]).astype(h_ref.dtype)


def _prenorm(x2, g_row, mod, tm=256):
    s, d = x2.shape
    return pl.pallas_call(
        _prenorm_kernel,
        grid=(s // tm,),
        in_specs=[
            pl.BlockSpec((tm, d), lambda i: (i, 0)),
            pl.BlockSpec((1, d), lambda i: (0, 0)),
            pl.BlockSpec((1, d), lambda i: (0, 0)),
            pl.BlockSpec((1, d), lambda i: (0, 1)),
        ],
        out_specs=pl.BlockSpec((tm, d), lambda i: (i, 0)),
        out_shape=jax.ShapeDtypeStruct((s, d), BF16),
        compiler_params=_params(1, [((tm, d), F32), ((tm, d), BF16)], [((tm, d), F32)] * 2),
        name="prenorm",
    )(x2, g_row, mod, mod)


def _mm_kernel(a_ref, b_ref, o_ref):
    o_ref[...] = jnp.dot(a_ref[...], b_ref[...], preferred_element_type=F32).astype(o_ref.dtype)


def _matmul(a, b, out_dtype, tm, tn, name):
    m, k = a.shape
    _, n = b.shape
    return pl.pallas_call(
        _mm_kernel,
        grid=(m // tm, n // tn),
        in_specs=[
            pl.BlockSpec((tm, k), lambda i, j: (i, 0)),
            pl.BlockSpec((k, tn), lambda i, j: (0, j)),
        ],
        out_specs=pl.BlockSpec((tm, tn), lambda i, j: (i, j)),
        out_shape=jax.ShapeDtypeStruct((m, n), out_dtype),
        compiler_params=_params(
            2, [((tm, k), a.dtype), ((k, tn), b.dtype), ((tm, tn), out_dtype)],
            [((tm, tn), F32), ((tm, tn), out_dtype)]),
        name=name,
    )(a, b)


def _first_half_mask():
    lane = lax.broadcasted_iota(jnp.int32, (1, LANES), 1)
    return (lane % MLA_DR) < (MLA_DR // 2)


def _low_group_mask():
    lane = lax.broadcasted_iota(jnp.int32, (1, LANES), 1)
    return lane < MLA_DR


def _rope_tables_kernel(pos_ref, invf_ref, c_ref, ss_ref):
    ang = pos_ref[...].astype(F32) * invf_ref[...]
    sin = jnp.sin(ang)
    c_ref[...] = jnp.cos(ang)
    ss_ref[...] = jnp.where(_first_half_mask(), -sin, sin)


def _rope_tables(pos_col, invf_row, tm=1024):
    s = pos_col.shape[0]
    return pl.pallas_call(
        _rope_tables_kernel,
        grid=(s // tm,),
        in_specs=[
            pl.BlockSpec((tm, 1), lambda i: (i, 0)),
            pl.BlockSpec((1, LANES), lambda i: (0, 0)),
        ],
        out_specs=[pl.BlockSpec((tm, LANES), lambda i: (i, 0))] * 2,
        out_shape=[jax.ShapeDtypeStruct((s, LANES), F32)] * 2,
        compiler_params=_params(1, [((tm, LANES), F32)] * 3, [((tm, LANES), F32)] * 4),
        name="rope_tables",
    )(pos_col, invf_row)


def _rope_group(x, cos, ss, first_half):
    partner = jnp.where(first_half,
                        pltpu.roll(x, LANES - MLA_DR // 2, 1),
                        pltpu.roll(x, MLA_DR // 2, 1))
    return x * cos + partner * ss


def _qproj_kernel(cq_ref, g_ref, w_ref, c_ref, ss_ref, q_ref, *, qscale):
    cqn = _rms_scale(cq_ref[...].astype(F32), g_ref[...]).astype(BF16)
    qb = jnp.dot(cqn, w_ref[...], preferred_element_type=F32) * qscale
    first_half = _first_half_mask()
    low = _low_group_mask()
    cos = c_ref[...]
    ss = ss_ref[...]
    rope0 = MLA_HEADS * MLA_DN
    for g in range(MLA_HEADS // 2):
        x = qb[:, rope0 + g * LANES: rope0 + (g + 1) * LANES]
        r = _rope_group(x, cos, ss, first_half)
        for hh in range(2):
            h = 2 * g + hh
            q_ref[h, :, 0:MLA_DN] = qb[:, h * MLA_DN:(h + 1) * MLA_DN].astype(q_ref.dtype)
            keep = low if hh == 0 else jnp.logical_not(low)
            q_ref[h, :, MLA_DN:MLA_DQK] = jnp.where(keep, r, 0.0).astype(q_ref.dtype)


def _qproj(proj, g_row, w_uq_p, cos, ss, qscale, tm=512):
    s = proj.shape[0]
    nq = w_uq_p.shape[1]
    cq_block = (MLA_HEADS * GLA_DK * 2 + 2 * GLA_WIDTH) // MLA_Q_RANK
    return pl.pallas_call(
        functools.partial(_qproj_kernel, qscale=qscale),
        grid=(s // tm,),
        in_specs=[
            pl.BlockSpec((tm, MLA_Q_RANK), lambda i: (i, cq_block)),
            pl.BlockSpec((1, MLA_Q_RANK), lambda i: (0, 0)),
            pl.BlockSpec((MLA_Q_RANK, nq), lambda i: (0, 0)),
            pl.BlockSpec((tm, LANES), lambda i: (i, 0)),
            pl.BlockSpec((tm, LANES), lambda i: (i, 0)),
        ],
        out_specs=pl.BlockSpec((MLA_HEADS, tm, MLA_DQK), lambda i: (0, i, 0)),
        out_shape=jax.ShapeDtypeStruct((MLA_HEADS, s, MLA_DQK), BF16),
        compiler_params=_params(
            1,
            [((tm, MLA_Q_RANK), BF16), ((MLA_Q_RANK, nq), BF16), ((tm, LANES), F32), ((tm, LANES), F32),
             ((MLA_HEADS, tm, MLA_DQK), BF16)],
            [((tm, nq), F32), ((tm, MLA_Q_RANK), F32)]),
        name="mla_qproj",
    )(proj, g_row, w_uq_p, cos, ss)


def _kvproj_kernel(ckv_ref, g_ref, w_ref, xs_ref, c_ref, ss_ref, k_ref, v_ref):
    kvn = _rms_scale(ckv_ref[...].astype(F32), g_ref[...]).astype(BF16)
    kvb = jnp.dot(kvn, w_ref[...], preferred_element_type=F32)
    r = _rope_group(xs_ref[...], c_ref[...], ss_ref[...], _first_half_mask())
    k_even = jnp.where(_low_group_mask(), r, 0.0)
    k_odd = pltpu.roll(k_even, MLA_DR, 1)
    hw = MLA_DN + MLA_DV
    for h in range(MLA_HEADS):
        k_ref[h, :, 0:MLA_DN] = kvb[:, h * hw: h * hw + MLA_DN].astype(k_ref.dtype)
        k_ref[h, :, MLA_DN:MLA_DQK] = (k_even if h % 2 == 0 else k_odd).astype(k_ref.dtype)
        v_ref[h] = kvb[:, h * hw + MLA_DN:(h + 1) * hw].astype(v_ref.dtype)


def _kvproj(proj, g_row, w_ukv_b, xs, cos, ss, tm=512):
    s = proj.shape[0]
    nkv = w_ukv_b.shape[1]
    ckv_block = (MLA_HEADS * GLA_DK * 2 + 2 * GLA_WIDTH + MLA_Q_RANK) // MLA_KV_RANK
    return pl.pallas_call(
        _kvproj_kernel,
        grid=(s // tm,),
        in_specs=[
            pl.BlockSpec((tm, MLA_KV_RANK), lambda i: (i, ckv_block)),
            pl.BlockSpec((1, MLA_KV_RANK), lambda i: (0, 0)),
            pl.BlockSpec((MLA_KV_RANK, nkv), lambda i: (0, 0)),
            pl.BlockSpec((tm, LANES), lambda i: (i, 0)),
            pl.BlockSpec((tm, LANES), lambda i: (i, 0)),
            pl.BlockSpec((tm, LANES), lambda i: (i, 0)),
        ],
        out_specs=[
            pl.BlockSpec((MLA_HEADS, tm, MLA_DQK), lambda i: (0, i, 0)),
            pl.BlockSpec((MLA_HEADS, tm, MLA_DV), lambda i: (0, i, 0)),
        ],
        out_shape=[
            jax.ShapeDtypeStruct((MLA_HEADS, s, MLA_DQK), BF16),
            jax.ShapeDtypeStruct((MLA_HEADS, s, MLA_DV), BF16),
        ],
        compiler_params=_params(
            1,
            [((tm, MLA_KV_RANK), BF16), ((MLA_KV_RANK, nkv), BF16), ((tm, LANES), F32), ((tm, LANES), F32),
             ((tm, LANES), F32), ((MLA_HEADS, tm, MLA_DQK), BF16), ((MLA_HEADS, tm, MLA_DV), BF16)],
            [((tm, nkv), F32)]),
        name="mla_kvproj",
    )(proj, g_row, w_ukv_b, xs, cos, ss)


def _flash_kernel(q_ref, k_ref, v_ref, g_ref, o_ref, m_scr, l_scr, acc_scr, *, tq):
    i = pl.program_id(1)
    q = q_ref[0]

    def step(row0, mask):
        kb = k_ref[0, pl.ds(row0, tq), :]
        vb = v_ref[0, pl.ds(row0, tq), :]
        s = lax.dot_general(q, kb, (((1,), (1,)), ((), ())), preferred_element_type=F32)
        if mask is not None:
            s = jnp.where(mask, s, -jnp.inf)
        m_old = m_scr[...]
        m_new = jnp.maximum(m_old, jnp.max(s, axis=1, keepdims=True))
        p = jnp.exp2(s - m_new)
        alpha = jnp.exp2(m_old - m_new)
        l_scr[...] = alpha * l_scr[...] + jnp.sum(p, axis=1, keepdims=True)
        acc_scr[...] = alpha * acc_scr[...] + jnp.dot(p.astype(BF16), vb, preferred_element_type=F32)
        m_scr[...] = m_new

    m_scr[...] = jnp.full(m_scr.shape, -jnp.inf, F32)
    l_scr[...] = jnp.zeros(l_scr.shape, F32)
    acc_scr[...] = jnp.zeros(acc_scr.shape, F32)

    row_chunk = lax.broadcasted_iota(jnp.int32, (tq, tq), 0) // CHUNK
    col_chunk = lax.broadcasted_iota(jnp.int32, (tq, tq), 1) // CHUNK
    step(pl.multiple_of(i * tq, tq), col_chunk <= row_chunk)

    def body(j, carry):
        step(pl.multiple_of(j * tq, tq), None)
        return carry

    lax.fori_loop(0, i, body, 0)

    gate = g_ref[...].astype(F32)
    o_ref[...] = (acc_scr[...] / l_scr[...] * _silu(gate)).astype(o_ref.dtype)


def _flash(q, k, v, proj, tq=512):
    h, s, _ = q.shape
    gate_block0 = (proj.shape[1] - MLA_WIDTH) // MLA_DV
    return pl.pallas_call(
        functools.partial(_flash_kernel, tq=tq),
        grid=(h, s // tq),
        in_specs=[
            pl.BlockSpec((1, tq, MLA_DQK), lambda hh, i: (hh, i, 0)),
            pl.BlockSpec((1, s, MLA_DQK), lambda hh, i: (hh, 0, 0)),
            pl.BlockSpec((1, s, MLA_DV), lambda hh, i: (hh, 0, 0)),
            pl.BlockSpec((tq, MLA_DV), lambda hh, i: (i, gate_block0 + hh)),
        ],
        out_specs=pl.BlockSpec((tq, MLA_DV), lambda hh, i: (i, hh)),
        out_shape=jax.ShapeDtypeStruct((s, h * MLA_DV), BF16),
        scratch_shapes=[
            pltpu.VMEM((tq, 1), F32),
            pltpu.VMEM((tq, 1), F32),
            pltpu.VMEM((tq, MLA_DV), F32),
        ],
        compiler_params=_params(
            2,
            [((tq, MLA_DQK), BF16), ((s, MLA_DQK), BF16), ((s, MLA_DV), BF16), ((tq, MLA_DV), BF16),
             ((tq, MLA_DV), BF16)],
            [((tq, tq), F32)] * 4 + [((tq, LANES), F32)] * 3),
        name="mla_flash",
    )(q, k, v, proj)


def _gla_kernel(q_ref, k_ref, v_ref, g_ref, xs_ref, wup_ref, b_ref, gn_ref, o_ref, st_scr, la_scr):
    tr = q_ref.shape[0]
    pair = 2 * GLA_DK
    pair_v = 2 * GLA_DV

    @pl.when(pl.program_id(0) == 0)
    def _():
        st_scr[...] = jnp.zeros(st_scr.shape, F32)

    z = jnp.dot(xs_ref[...].astype(BF16), wup_ref[...], preferred_element_type=F32) + b_ref[...]
    la_scr[...] = (jnp.minimum(z, 0.0) - jnp.log1p(jnp.exp(-jnp.abs(z)))) * (1.0 / GLA_GATE_TAU)

    rr = lax.broadcasted_iota(jnp.int32, (CHUNK, CHUNK), 0)
    cc = lax.broadcasted_iota(jnp.int32, (CHUNK, CHUNK), 1)
    tri = (cc <= rr).astype(BF16)
    low = _low_group_mask()
    gn = gn_ref[...]

    def chunk(ci, carry):
        rows = pl.ds(pl.multiple_of(ci * CHUNK, CHUNK), CHUNK)
        la = la_scr[rows, :]
        la_hi = la.astype(BF16)
        la_lo = (la - la_hi.astype(F32)).astype(BF16)
        lcum = (jnp.dot(tri, la_hi, preferred_element_type=F32)
                + jnp.dot(tri, la_lo, preferred_element_type=F32))
        ltot = lcum[CHUNK - 1:CHUNK, :]
        kdec = (k_ref[rows, :].astype(F32) * jnp.exp(ltot - lcum)).astype(BF16)
        dec = jnp.exp(ltot)
        qc = q_ref[rows, :]
        vc = v_ref[rows, :]
        gc = g_ref[rows, :]
        for p in range(GLA_HEADS // 2):
            lanes = slice(p * pair, (p + 1) * pair)
            cross = lax.dot_general(vc[:, p * pair_v:(p + 1) * pair_v], kdec[:, lanes],
                                    (((0,), (0,)), ((), ())), preferred_element_type=F32)
            kvt = jnp.where(low, cross[:GLA_DV], cross[GLA_DV:])
            st = st_scr[:, lanes] * dec[:, lanes] + kvt
            st_scr[:, lanes] = st
            qp = qc[:, lanes]
            zero = jnp.zeros_like(qp)
            lhs = jnp.concatenate([jnp.where(low, qp, zero), jnp.where(low, zero, qp)], axis=0)
            o2 = lax.dot_general(lhs, st.astype(BF16), (((1,), (1,)), ((), ())),
                                 preferred_element_type=F32)
            for hh in range(2):
                cols = slice((2 * p + hh) * GLA_DV, (2 * p + hh + 1) * GLA_DV)
                y = o2[hh * CHUNK:(hh + 1) * CHUNK] * (GLA_DK ** -0.5)
                gate = gc[:, cols].astype(F32)
                o_ref[rows, cols] = (_rms_scale(y, gn) * _silu(gate)).astype(o_ref.dtype)
        return carry

    lax.fori_loop(0, tr // CHUNK, chunk, 0)


def _gla(proj, xs, wup, b_row, gn_row, tr=512):
    s = proj.shape[0]
    qk = GLA_HEADS * GLA_DK
    return pl.pallas_call(
        _gla_kernel,
        grid=(s // tr,),
        in_specs=[
            pl.BlockSpec((tr, qk), lambda i: (i, 0)),
            pl.BlockSpec((tr, qk), lambda i: (i, 1)),
            pl.BlockSpec((tr, GLA_WIDTH), lambda i: (i, 1)),
            pl.BlockSpec((tr, GLA_WIDTH), lambda i: (i, 2)),
            pl.BlockSpec((tr, LANES), lambda i: (i, 0)),
            pl.BlockSpec((LANES, qk), lambda i: (0, 0)),
            pl.BlockSpec((1, qk), lambda i: (0, 0)),
            pl.BlockSpec((1, GLA_DV), lambda i: (0, 0)),
        ],
        out_specs=pl.BlockSpec((tr, GLA_WIDTH), lambda i: (i, 0)),
        out_shape=jax.ShapeDtypeStruct((s, GLA_WIDTH), BF16),
        scratch_shapes=[
            pltpu.VMEM((GLA_DV, qk), F32),
            pltpu.VMEM((tr, qk), F32),
        ],
        compiler_params=_params(
            1,
            [((tr, qk), BF16)] * 2 + [((tr, GLA_WIDTH), BF16)] * 3 + [((tr, LANES), F32), ((LANES, qk), BF16)],
            [((GLA_DV, qk), F32), ((tr, qk), F32), ((tr, qk), F32)]),
        name="gla",
    )(proj, proj, proj, proj, xs, wup, b_row, gn_row)


def _outproj_kernel(a1_ref, a2_ref, w1_ref, w2_ref, o_ref):
    o_ref[...] = (jnp.dot(a1_ref[...], w1_ref[...], preferred_element_type=F32)
                  + jnp.dot(a2_ref[...], w2_ref[...], preferred_element_type=F32))


def _outproj(o_a, o_b, w_out_b, tm=1024, tn=1024):
    s, ka = o_a.shape
    kb = o_b.shape[1]
    n = w_out_b.shape[1]
    assert ka == kb
    return pl.pallas_call(
        _outproj_kernel,
        grid=(s // tm, n // tn),
        in_specs=[
            pl.BlockSpec((tm, ka), lambda i, j: (i, 0)),
            pl.BlockSpec((tm, kb), lambda i, j: (i, 0)),
            pl.BlockSpec((ka, tn), lambda i, j: (0, j)),
            pl.BlockSpec((kb, tn), lambda i, j: (1, j)),
        ],
        out_specs=pl.BlockSpec((tm, tn), lambda i, j: (i, j)),
        out_shape=jax.ShapeDtypeStruct((s, n), F32),
        compiler_params=_params(
            2, [((tm, ka), BF16)] * 2 + [((ka, tn), BF16)] * 2 + [((tm, tn), F32)], [((tm, tn), F32)] * 3),
        name="outproj",
    )(o_a, o_b, w_out_b, w_out_b)


def _postnorm_kernel(mix_ref, x_ref, g_ref, gate_ref, o_ref):
    o_ref[...] = x_ref[...] + gate_ref[...] * _rms_scale(mix_ref[...], g_ref[...])


def _postnorm(mix, x2, g_row, mod, tm=256):
    s, d = x2.shape
    return pl.pallas_call(
        _postnorm_kernel,
        grid=(s // tm,),
        in_specs=[
            pl.BlockSpec((tm, d), lambda i: (i, 0)),
            pl.BlockSpec((tm, d), lambda i: (i, 0)),
            pl.BlockSpec((1, d), lambda i: (0, 0)),
            pl.BlockSpec((1, d), lambda i: (0, 2)),
        ],
        out_specs=pl.BlockSpec((tm, d), lambda i: (i, 0)),
        out_shape=jax.ShapeDtypeStruct((s, d), F32),
        compiler_params=_params(1, [((tm, d), F32)] * 3, [((tm, d), F32)] * 2),
        name="postnorm",
    )(mix, x2, g_row, mod)


def kernel(x, c, positions, w_ada, b_ada, g_pre, g_post, w_in, w_alpha_up, b_alpha,
           g_gla_out, g_q_norm, w_uq, g_kv_norm, w_ukv, w_out):
    b, s, d = x.shape
    assert b == 1, "single-sequence kernel"
    x2 = x.reshape(s, d)

    qk = GLA_HEADS * GLA_DK
    o_alr = 2 * qk + GLA_WIDTH
    o_gate = o_alr + GLA_GATE_RANK
    o_kr = o_gate + GLA_WIDTH + MLA_Q_RANK + MLA_KV_RANK
    o_mgate = o_kr + MLA_DR
    w_main = jnp.concatenate([w_in[:, :o_alr], w_in[:, o_gate:o_kr], w_in[:, o_mgate:]], axis=1).astype(BF16)
    pad = LANES - MLA_DR - GLA_GATE_RANK
    w_small = jnp.concatenate(
        [w_in[:, o_kr:o_mgate], w_in[:, o_alr:o_gate], jnp.zeros((d, pad), w_in.dtype)], axis=1).astype(BF16)
    wup = jnp.zeros((LANES, qk), F32).at[MLA_DR:MLA_DR + GLA_GATE_RANK].set(w_alpha_up).astype(BF16)
    w_uq3 = w_uq.reshape(MLA_Q_RANK, MLA_HEADS, MLA_DN + MLA_DR)
    w_uq_p = jnp.concatenate(
        [w_uq3[:, :, :MLA_DN].reshape(MLA_Q_RANK, MLA_HEADS * MLA_DN),
         w_uq3[:, :, MLA_DN:].reshape(MLA_Q_RANK, MLA_HEADS * MLA_DR)], axis=1).astype(BF16)
    w_ukv_b = w_ukv.astype(BF16)
    w_out_b = w_out.astype(BF16)
    inv_freq = ROPE_THETA ** (-jnp.arange(MLA_DR // 2, dtype=F32) / (MLA_DR // 2))
    invf_row = jnp.tile(inv_freq, LANES // (MLA_DR // 2)).reshape(1, LANES)

    mod = _adaln(c.reshape(d, 1), w_ada, b_ada.reshape(1, -1))
    h = _prenorm(x2, g_pre.reshape(1, d), mod)

    proj = _matmul(h, w_main, BF16, 1024, 1024, "inproj")
    xs = _matmul(h, w_small, F32, 1024, LANES, "inproj_small")

    o_a = _gla(proj, xs, wup, b_alpha.reshape(1, qk), g_gla_out.reshape(1, GLA_DV))

    cos, ss = _rope_tables(positions.reshape(s, 1), invf_row)
    qscale = (MLA_DN + MLA_DR) ** -0.5 * math.log2(math.e)
    q = _qproj(proj, g_q_norm.reshape(1, MLA_Q_RANK), w_uq_p, cos, ss, qscale)
    k, v = _kvproj(proj, g_kv_norm.reshape(1, MLA_KV_RANK), w_ukv_b, xs, cos, ss)
    o_b = _flash(q, k, v, proj)

    mix = _outproj(o_a, o_b, w_out_b)
    out = _postnorm(mix, x2, g_post.reshape(1, d), mod)
    return out.reshape(b, s, d)
```

```python
import functools
import math

import jax
import jax.numpy as jnp
from jax import lax
from jax.experimental import pallas as pl
from jax.experimental.pallas import tpu as pltpu

CHUNK = 64

GLA_HEADS = 16
GLA_DK = 64
GLA_DV = 128
GLA_WIDTH = GLA_HEADS * GLA_DV
GLA_GATE_RANK = 16
GLA_GATE_TAU = 16.0

MLA_HEADS = 16
MLA_DN = 128
MLA_DR = 64
MLA_DV = 128
MLA_WIDTH = MLA_HEADS * MLA_DV
MLA_Q_RANK = 1536
MLA_KV_RANK = 512
MLA_DQK = 256
FLASH_TQ = 1024
FLASH_TK = 512
MASK_BIAS = -1e30
ROPE_THETA = 10000.0
EPS = 1e-6

LANES = 128
V7X_SCOPED_VMEM_BYTES = 60000 * 1024

F32 = jnp.float32
BF16 = jnp.bfloat16


def _nbytes(shape, dtype):
    return math.prod(shape) * jnp.dtype(dtype).itemsize


def _params(n_axes, blocks, temps=()):
    need = 2 * sum(_nbytes(s, d) for s, d in blocks) + sum(_nbytes(s, d) for s, d in temps)
    need += 2 * 1024 * 1024
    return pltpu.CompilerParams(
        dimension_semantics=("arbitrary",) * n_axes,
        vmem_limit_bytes=min(need, V7X_SCOPED_VMEM_BYTES),
    )


def _silu(t):
    return t * jax.nn.sigmoid(t)


def _rms_scale(t, g):
    ms = jnp.mean(t * t, axis=-1, keepdims=True)
    return t * lax.rsqrt(ms + EPS) * g


_ADALN_ROWS = 256


def _adaln_kernel(c_ref, w_ref, b_ref, o_ref):
    d = w_ref.shape[0]
    tn = w_ref.shape[1]

    def body(r, acc):
        rows = pl.ds(pl.multiple_of(r * _ADALN_ROWS, _ADALN_ROWS), _ADALN_ROWS)
        prod = w_ref[rows, :] * _silu(c_ref[rows, :])
        return acc + jnp.sum(prod.reshape(_ADALN_ROWS // 8, 8, tn), axis=0)

    acc = lax.fori_loop(0, d // _ADALN_ROWS, body, jnp.zeros((8, tn), F32))
    o_ref[...] = jnp.sum(acc, axis=0, keepdims=True) + b_ref[...]


def _adaln(c_col, w_ada, b_row, tn=512):
    d, n = w_ada.shape
    return pl.pallas_call(
        _adaln_kernel,
        grid=(n // tn,),
        in_specs=[
            pl.BlockSpec((d, 1), lambda j: (0, 0)),
            pl.BlockSpec((d, tn), lambda j: (0, j)),
            pl.BlockSpec((1, tn), lambda j: (0, j)),
        ],
        out_specs=pl.BlockSpec((1, tn), lambda j: (0, j)),
        out_shape=jax.ShapeDtypeStruct((1, n), F32),
        compiler_params=_params(1, [((d, LANES), F32), ((d, tn), F32)]),
        name="adaln",
    )(c_col, w_ada, b_row)


def _prenorm_kernel(x_ref, g_ref, shift_ref, scale_ref, h_ref):
    y = _rms_scale(x_ref[...], g_ref[...])
    h_ref[...] = (y * (1.0 + scale_ref[...]) + shift_ref[...]).astype(h_ref.dtype)


def _prenorm(x2, g_row, mod, tm=256):
    s, d = x2.shape
    return pl.pallas_call(
        _prenorm_kernel,
        grid=(s // tm,),
        in_specs=[
            pl.BlockSpec((tm, d), lambda i: (i, 0)),
            pl.BlockSpec((1, d), lambda i: (0, 0)),
            pl.BlockSpec((1, d), lambda i: (0, 0)),
            pl.BlockSpec((1, d), lambda i: (0, 1)),
        ],
        out_specs=pl.BlockSpec((tm, d), lambda i: (i, 0)),
        out_shape=jax.ShapeDtypeStruct((s, d), BF16),
        compiler_params=_params(1, [((tm, d), F32), ((tm, d), BF16)], [((tm, d), F32)] * 2),
        name="prenorm",
    )(x2, g_row, mod, mod)


def _mm_kernel(a_ref, b_ref, o_ref):
    o_ref[...] = jnp.dot(a_ref[...], b_ref[...], preferred_element_type=F32).astype(o_ref.dtype)


def _matmul(a, b, out_dtype, tm, tn, name):
    m, k = a.shape
    _, n = b.shape
    return pl.pallas_call(
        _mm_kernel,
        grid=(m // tm, n // tn),
        in_specs=[
            pl.BlockSpec((tm, k), lambda i, j: (i, 0)),
            pl.BlockSpec((k, tn), lambda i, j: (0, j)),
        ],
        out_specs=pl.BlockSpec((tm, tn), lambda i, j: (i, j)),
        out_shape=jax.ShapeDtypeStruct((m, n), out_dtype),
        compiler_params=_params(
            2, [((tm, k), a.dtype), ((k, tn), b.dtype), ((tm, tn), out_dtype)],
            [((tm, tn), F32), ((tm, tn), out_dtype)]),
        name=name,
    )(a, b)


def _first_half_mask():
    lane = lax.broadcasted_iota(jnp.int32, (1, LANES), 1)
    return (lane % MLA_DR) < (MLA_DR // 2)


def _low_group_mask():
    lane = lax.broadcasted_iota(jnp.int32, (1, LANES), 1)
    return lane < MLA_DR


def _rope_tables_kernel(pos_col_ref, pos_row_ref, invf_row_ref, invf_col_ref, c_ref, ss_ref, ct_ref, st_ref):
    ang = pos_col_ref[...].astype(F32) * invf_row_ref[...]
    sin = jnp.sin(ang)
    c_ref[...] = jnp.cos(ang)
    ss_ref[...] = jnp.where(_first_half_mask(), -sin, sin)
    ang_t = pos_row_ref[...].astype(F32) * invf_col_ref[...]
    ct_ref[...] = jnp.cos(ang_t)
    st_ref[...] = jnp.sin(ang_t)


def _rope_tables(pos_col, pos_row, invf_row, invf_col, tm=1024):
    s = pos_col.shape[0]
    half = invf_col.shape[0]
    return pl.pallas_call(
        _rope_tables_kernel,
        grid=(s // tm,),
        in_specs=[
            pl.BlockSpec((tm, 1), lambda i: (i, 0)),
            pl.BlockSpec((1, tm), lambda i: (0, i)),
            pl.BlockSpec((1, LANES), lambda i: (0, 0)),
            pl.BlockSpec((half, 1), lambda i: (0, 0)),
        ],
        out_specs=[pl.BlockSpec((tm, LANES), lambda i: (i, 0))] * 2
        + [pl.BlockSpec((half, tm), lambda i: (0, i))] * 2,
        out_shape=[jax.ShapeDtypeStruct((s, LANES), F32)] * 2
        + [jax.ShapeDtypeStruct((half, s), F32)] * 2,
        compiler_params=_params(1, [((tm, LANES), F32)] * 4, [((tm, LANES), F32)] * 6),
        name="rope_tables",
    )(pos_col, pos_row, invf_row, invf_col)


def _rope_group(x, cos, ss, first_half):
    partner = jnp.where(first_half,
                        pltpu.roll(x, LANES - MLA_DR // 2, 1),
                        pltpu.roll(x, MLA_DR // 2, 1))
    return x * cos + partner * ss


def _qproj_kernel(cq_ref, g_ref, wt_ref, ct_ref, st_ref, qt_ref, *, qscale):
    cqn = _rms_scale(cq_ref[...].astype(F32), g_ref[...]).astype(BF16)
    qbt = lax.dot_general(wt_ref[...], cqn, (((1,), (1,)), ((), ())),
                          preferred_element_type=F32) * qscale
    cos = ct_ref[...]
    sin = st_ref[...]
    half = MLA_DR // 2
    rope0 = MLA_HEADS * MLA_DN
    tm = qbt.shape[1]
    n_aug = MLA_DQK - MLA_DN - MLA_DR
    c_row = lax.broadcasted_iota(jnp.int32, (n_aug, tm), 0)
    col = pl.program_id(0) * tm + lax.broadcasted_iota(jnp.int32, (n_aug, tm), 1)
    qry_chunk = (col % FLASH_TQ) // CHUNK
    zeros = jnp.where((c_row > qry_chunk) & (c_row < FLASH_TQ // CHUNK), MASK_BIAS, 0.0).astype(qt_ref.dtype)
    for h in range(MLA_HEADS):
        qt_ref[h, 0:MLA_DN, :] = qbt[h * MLA_DN:(h + 1) * MLA_DN].astype(qt_ref.dtype)
        t1 = qbt[rope0 + h * MLA_DR: rope0 + h * MLA_DR + half]
        t2 = qbt[rope0 + h * MLA_DR + half: rope0 + (h + 1) * MLA_DR]
        qt_ref[h, MLA_DN:MLA_DN + half, :] = (t1 * cos - t2 * sin).astype(qt_ref.dtype)
        qt_ref[h, MLA_DN + half:MLA_DN + MLA_DR, :] = (t2 * cos + t1 * sin).astype(qt_ref.dtype)
        qt_ref[h, MLA_DN + MLA_DR:MLA_DQK, :] = zeros


def _qproj(proj, g_row, w_uq_t, cos_t, sin_t, qscale, tm=512):
    s = proj.shape[0]
    nq = w_uq_t.shape[0]
    half = cos_t.shape[0]
    cq_block = (MLA_HEADS * GLA_DK * 2 + 2 * GLA_WIDTH) // MLA_Q_RANK
    return pl.pallas_call(
        functools.partial(_qproj_kernel, qscale=qscale),
        grid=(s // tm,),
        in_specs=[
            pl.BlockSpec((tm, MLA_Q_RANK), lambda i: (i, cq_block)),
            pl.BlockSpec((1, MLA_Q_RANK), lambda i: (0, 0)),
            pl.BlockSpec((nq, MLA_Q_RANK), lambda i: (0, 0)),
            pl.BlockSpec((half, tm), lambda i: (0, i)),
            pl.BlockSpec((half, tm), lambda i: (0, i)),
        ],
        out_specs=pl.BlockSpec((MLA_HEADS, MLA_DQK, tm), lambda i: (0, 0, i)),
        out_shape=jax.ShapeDtypeStruct((MLA_HEADS, MLA_DQK, s), BF16),
        compiler_params=_params(
            1,
            [((tm, MLA_Q_RANK), BF16), ((nq, MLA_Q_RANK), BF16), ((half, tm), F32), ((half, tm), F32),
             ((MLA_HEADS, MLA_DQK, tm), BF16)],
            [((nq, tm), F32), ((nq, tm), F32), ((tm, MLA_Q_RANK), F32)]),
        name="mla_qproj",
    )(proj, g_row, w_uq_t, cos_t, sin_t)


def _kvproj_kernel(ckv_ref, g_ref, wk_ref, wvt_ref, xs_ref, c_ref, ss_ref, k_ref, vt_ref):
    kvn = _rms_scale(ckv_ref[...].astype(F32), g_ref[...]).astype(BF16)
    kk = jnp.dot(kvn, wk_ref[...], preferred_element_type=F32)
    vt = lax.dot_general(wvt_ref[...], kvn, (((1,), (1,)), ((), ())),
                         preferred_element_type=F32)
    r = _rope_group(xs_ref[...], c_ref[...], ss_ref[...], _first_half_mask())
    kr = jnp.where(_low_group_mask(), r, 0.0)
    tm = kr.shape[0]
    row = pl.program_id(0) * tm + lax.broadcasted_iota(jnp.int32, (tm, LANES), 0)
    lane = lax.broadcasted_iota(jnp.int32, (tm, LANES), 1)
    kr_aug = jnp.where(lane == MLA_DR + (row % FLASH_TQ) // CHUNK, 1.0, kr).astype(k_ref.dtype)
    kr = kr.astype(k_ref.dtype)
    for h in range(MLA_HEADS):
        kn = kk[:, h * MLA_DN:(h + 1) * MLA_DN].astype(k_ref.dtype)
        k_ref[h, 0, :, 0:MLA_DN] = kn
        k_ref[h, 0, :, MLA_DN:MLA_DQK] = kr
        k_ref[h, 1, :, 0:MLA_DN] = kn
        k_ref[h, 1, :, MLA_DN:MLA_DQK] = kr_aug
        vt_ref[h] = vt[h * MLA_DV:(h + 1) * MLA_DV].astype(vt_ref.dtype)


def _kvproj(proj, g_row, w_k, w_vt, xs, cos, ss, tm=512):
    s = proj.shape[0]
    nk = w_k.shape[1]
    nv = w_vt.shape[0]
    ckv_block = (MLA_HEADS * GLA_DK * 2 + 2 * GLA_WIDTH + MLA_Q_RANK) // MLA_KV_RANK
    return pl.pallas_call(
        _kvproj_kernel,
        grid=(s // tm,),
        in_specs=[
            pl.BlockSpec((tm, MLA_KV_RANK), lambda i: (i, ckv_block)),
            pl.BlockSpec((1, MLA_KV_RANK), lambda i: (0, 0)),
            pl.BlockSpec((MLA_KV_RANK, nk), lambda i: (0, 0)),
            pl.BlockSpec((nv, MLA_KV_RANK), lambda i: (0, 0)),
            pl.BlockSpec((tm, LANES), lambda i: (i, 0)),
            pl.BlockSpec((tm, LANES), lambda i: (i, 0)),
            pl.BlockSpec((tm, LANES), lambda i: (i, 0)),
        ],
        out_specs=[
            pl.BlockSpec((MLA_HEADS, 2, tm, MLA_DQK), lambda i: (0, 0, i, 0)),
            pl.BlockSpec((MLA_HEADS, MLA_DV, tm), lambda i: (0, 0, i)),
        ],
        out_shape=[
            jax.ShapeDtypeStruct((MLA_HEADS, 2, s, MLA_DQK), BF16),
            jax.ShapeDtypeStruct((MLA_HEADS, MLA_DV, s), BF16),
        ],
        compiler_params=_params(
            1,
            [((tm, MLA_KV_RANK), BF16), ((MLA_KV_RANK, nk), BF16), ((nv, MLA_KV_RANK), BF16),
             ((tm, LANES), F32), ((tm, LANES), F32), ((tm, LANES), F32),
             ((MLA_HEADS, 2, tm, MLA_DQK), BF16), ((MLA_HEADS, MLA_DV, tm), BF16)],
            [((tm, nk), F32), ((nv, tm), F32)]),
        name="mla_kvproj",
    )(proj, g_row, w_k, w_vt, xs, cos, ss)


def _flash_kernel(qt_ref, kk_ref, vt_ref, g_ref, o_ref, sa_scr, sb_scr, m_scr, l_scr, acc_scr, *, tq, tk):
    i = pl.program_id(1)
    r = tq // tk
    n = r * (i + 1)
    qt = qt_ref[0]

    def scores(j, dst):
        j = jnp.minimum(j, n - 1)
        diag = (j >= r * i).astype(jnp.int32)
        kb = kk_ref[0, diag, pl.ds(pl.multiple_of(j * tk, tk), tk), :]
        dst[...] = jnp.dot(kb, qt, preferred_element_type=F32)

    def absorb(j, src):
        s = src[...]
        m_old = m_scr[...]
        m_new = jnp.maximum(m_old, jnp.max(s, axis=0, keepdims=True))
        p = jnp.exp2(s - m_new)
        alpha = jnp.exp2(m_old - m_new)
        l_scr[...] = alpha * l_scr[...] + jnp.sum(p, axis=0, keepdims=True)
        vtb = vt_ref[0, :, pl.ds(pl.multiple_of(j * tk, tk), tk)]
        acc_scr[...] = alpha * acc_scr[...] + jnp.dot(vtb, p.astype(BF16), preferred_element_type=F32)
        m_scr[...] = m_new

    m_scr[...] = jnp.full(m_scr.shape, -jnp.inf, F32)
    l_scr[...] = jnp.zeros(l_scr.shape, F32)
    acc_scr[...] = jnp.zeros(acc_scr.shape, F32)
    scores(0, sa_scr)

    def body(p, carry):
        scores(2 * p + 1, sb_scr)
        absorb(2 * p, sa_scr)
        scores(2 * p + 2, sa_scr)
        absorb(2 * p + 1, sb_scr)
        return carry

    lax.fori_loop(0, n // 2, body, 0)

    gate = g_ref[...].astype(F32)
    o_t = acc_scr[...] / l_scr[...]
    o_ref[...] = (o_t.T * _silu(gate)).astype(o_ref.dtype)


def _flash(qt, kk, vt, proj, tq=FLASH_TQ, tk=FLASH_TK):
    h, _, s, _ = kk.shape
    assert (tq // tk) % 2 == 0 and tq % tk == 0
    gate_block0 = (proj.shape[1] - MLA_WIDTH) // MLA_DV
    return pl.pallas_call(
        functools.partial(_flash_kernel, tq=tq, tk=tk),
        grid=(h, s // tq),
        in_specs=[
            pl.BlockSpec((1, MLA_DQK, tq), lambda hh, i: (hh, 0, i)),
            pl.BlockSpec((1, 2, s, MLA_DQK), lambda hh, i: (hh, 0, 0, 0)),
            pl.BlockSpec((1, MLA_DV, s), lambda hh, i: (hh, 0, 0)),
            pl.BlockSpec((tq, MLA_DV), lambda hh, i: (i, gate_block0 + hh)),
        ],
        out_specs=pl.BlockSpec((tq, MLA_DV), lambda hh, i: (i, hh)),
        out_shape=jax.ShapeDtypeStruct((s, h * MLA_DV), BF16),
        scratch_shapes=[
            pltpu.VMEM((tk, tq), F32),
            pltpu.VMEM((tk, tq), F32),
            pltpu.VMEM((1, tq), F32),
            pltpu.VMEM((1, tq), F32),
            pltpu.VMEM((MLA_DV, tq), F32),
        ],
        compiler_params=_params(
            2,
            [((MLA_DQK, tq), BF16), ((2, s, MLA_DQK), BF16), ((MLA_DV, s), BF16), ((tq, MLA_DV), BF16),
             ((tq, MLA_DV), BF16)],
            [((tk, tq), F32)] * 7),
        name="mla_flash",
    )(qt, kk, vt, proj)


def _gla_kernel(q_ref, k_ref, v_ref, g_ref, xs_ref, wup_ref, b_ref, gn_ref, o_ref, st_scr, la_scr):
    tr = q_ref.shape[0]
    pair = 2 * GLA_DK
    pair_v = 2 * GLA_DV

    @pl.when(pl.program_id(0) == 0)
    def _():
        st_scr[...] = jnp.zeros(st_scr.shape, F32)

    z = jnp.dot(xs_ref[...].astype(BF16), wup_ref[...], preferred_element_type=F32) + b_ref[...]
    la_scr[...] = (jnp.minimum(z, 0.0) - jnp.log1p(jnp.exp(-jnp.abs(z)))) * (1.0 / GLA_GATE_TAU)

    rr = lax.broadcasted_iota(jnp.int32, (CHUNK, CHUNK), 0)
    cc = lax.broadcasted_iota(jnp.int32, (CHUNK, CHUNK), 1)
    tri = (cc <= rr).astype(BF16)
    low = _low_group_mask()
    gn = gn_ref[...]

    def chunk(ci, carry):
        rows = pl.ds(pl.multiple_of(ci * CHUNK, CHUNK), CHUNK)
        la = la_scr[rows, :]
        la_hi = la.astype(BF16)
        la_lo = (la - la_hi.astype(F32)).astype(BF16)
        lcum = (jnp.dot(tri, la_hi, preferred_element_type=F32)
                + jnp.dot(tri, la_lo, preferred_element_type=F32))
        ltot = lcum[CHUNK - 1:CHUNK, :]
        kdec = (k_ref[rows, :].astype(F32) * jnp.exp(ltot - lcum)).astype(BF16)
        dec = jnp.exp(ltot)
        qc = q_ref[rows, :]
        vc = v_ref[rows, :]
        gc = g_ref[rows, :]
        for p in range(GLA_HEADS // 2):
            lanes = slice(p * pair, (p + 1) * pair)
            cross = lax.dot_general(vc[:, p * pair_v:(p + 1) * pair_v], kdec[:, lanes],
                                    (((0,), (0,)), ((), ())), preferred_element_type=F32)
            kvt = jnp.where(low, cross[:GLA_DV], cross[GLA_DV:])
            st = st_scr[:, lanes] * dec[:, lanes] + kvt
            st_scr[:, lanes] = st
            qp = qc[:, lanes]
            zero = jnp.zeros_like(qp)
            lhs = jnp.concatenate([jnp.where(low, qp, zero), jnp.where(low, zero, qp)], axis=0)
            o2 = lax.dot_general(lhs, st.astype(BF16), (((1,), (1,)), ((), ())),
                                 preferred_element_type=F32)
            for hh in range(2):
                cols = slice((2 * p + hh) * GLA_DV, (2 * p + hh + 1) * GLA_DV)
                y = o2[hh * CHUNK:(hh + 1) * CHUNK] * (GLA_DK ** -0.5)
                gate = gc[:, cols].astype(F32)
                o_ref[rows, cols] = (_rms_scale(y, gn) * _silu(gate)).astype(o_ref.dtype)
        return carry

    lax.fori_loop(0, tr // CHUNK, chunk, 0)


def _gla(proj, xs, wup, b_row, gn_row, tr=512):
    s = proj.shape[0]
    qk = GLA_HEADS * GLA_DK
    return pl.pallas_call(
        _gla_kernel,
        grid=(s // tr,),
        in_specs=[
            pl.BlockSpec((tr, qk), lambda i: (i, 0)),
            pl.BlockSpec((tr, qk), lambda i: (i, 1)),
            pl.BlockSpec((tr, GLA_WIDTH), lambda i: (i, 1)),
            pl.BlockSpec((tr, GLA_WIDTH), lambda i: (i, 2)),
            pl.BlockSpec((tr, LANES), lambda i: (i, 0)),
            pl.BlockSpec((LANES, qk), lambda i: (0, 0)),
            pl.BlockSpec((1, qk), lambda i: (0, 0)),
            pl.BlockSpec((1, GLA_DV), lambda i: (0, 0)),
        ],
        out_specs=pl.BlockSpec((tr, GLA_WIDTH), lambda i: (i, 0)),
        out_shape=jax.ShapeDtypeStruct((s, GLA_WIDTH), BF16),
        scratch_shapes=[
            pltpu.VMEM((GLA_DV, qk), F32),
            pltpu.VMEM((tr, qk), F32),
        ],
        compiler_params=_params(
            1,
            [((tr, qk), BF16)] * 2 + [((tr, GLA_WIDTH), BF16)] * 3 + [((tr, LANES), F32), ((LANES, qk), BF16)],
            [((GLA_DV, qk), F32), ((tr, qk), F32), ((tr, qk), F32)]),
        name="gla",
    )(proj, proj, proj, proj, xs, wup, b_row, gn_row)


def _outproj_kernel(a1_ref, a2_ref, w1_ref, w2_ref, o_ref):
    o_ref[...] = (jnp.dot(a1_ref[...], w1_ref[...], preferred_element_type=F32)
                  + jnp.dot(a2_ref[...], w2_ref[...], preferred_element_type=F32))


def _outproj(o_a, o_b, w_out_b, tm=1024, tn=1024):
    s, ka = o_a.shape
    kb = o_b.shape[1]
    n = w_out_b.shape[1]
    assert ka == kb
    return pl.pallas_call(
        _outproj_kernel,
        grid=(s // tm, n // tn),
        in_specs=[
            pl.BlockSpec((tm, ka), lambda i, j: (i, 0)),
            pl.BlockSpec((tm, kb), lambda i, j: (i, 0)),
            pl.BlockSpec((ka, tn), lambda i, j: (0, j)),
            pl.BlockSpec((kb, tn), lambda i, j: (1, j)),
        ],
        out_specs=pl.BlockSpec((tm, tn), lambda i, j: (i, j)),
        out_shape=jax.ShapeDtypeStruct((s, n), F32),
        compiler_params=_params(
            2, [((tm, ka), BF16)] * 2 + [((ka, tn), BF16)] * 2 + [((tm, tn), F32)], [((tm, tn), F32)] * 3),
        name="outproj",
    )(o_a, o_b, w_out_b, w_out_b)


def _postnorm_kernel(mix_ref, x_ref, g_ref, gate_ref, o_ref):
    o_ref[...] = x_ref[...] + gate_ref[...] * _rms_scale(mix_ref[...], g_ref[...])


def _postnorm(mix, x2, g_row, mod, tm=256):
    s, d = x2.shape
    return pl.pallas_call(
        _postnorm_kernel,
        grid=(s // tm,),
        in_specs=[
            pl.BlockSpec((tm, d), lambda i: (i, 0)),
            pl.BlockSpec((tm, d), lambda i: (i, 0)),
            pl.BlockSpec((1, d), lambda i: (0, 0)),
            pl.BlockSpec((1, d), lambda i: (0, 2)),
        ],
        out_specs=pl.BlockSpec((tm, d), lambda i: (i, 0)),
        out_shape=jax.ShapeDtypeStruct((s, d), F32),
        compiler_params=_params(1, [((tm, d), F32)] * 3, [((tm, d), F32)] * 2),
        name="postnorm",
    )(mix, x2, g_row, mod)


def kernel(x, c, positions, w_ada, b_ada, g_pre, g_post, w_in, w_alpha_up, b_alpha,
           g_gla_out, g_q_norm, w_uq, g_kv_norm, w_ukv, w_out):
    b, s, d = x.shape
    assert b == 1, "single-sequence kernel"
    x2 = x.reshape(s, d)

    qk = GLA_HEADS * GLA_DK
    o_alr = 2 * qk + GLA_WIDTH
    o_gate = o_alr + GLA_GATE_RANK
    o_kr = o_gate + GLA_WIDTH + MLA_Q_RANK + MLA_KV_RANK
    o_mgate = o_kr + MLA_DR
    w_main = jnp.concatenate([w_in[:, :o_alr], w_in[:, o_gate:o_kr], w_in[:, o_mgate:]], axis=1).astype(BF16)
    pad = LANES - MLA_DR - GLA_GATE_RANK
    w_small = jnp.concatenate(
        [w_in[:, o_kr:o_mgate], w_in[:, o_alr:o_gate], jnp.zeros((d, pad), w_in.dtype)], axis=1).astype(BF16)
    wup = jnp.zeros((LANES, qk), F32).at[MLA_DR:MLA_DR + GLA_GATE_RANK].set(w_alpha_up).astype(BF16)
    w_uq3 = w_uq.reshape(MLA_Q_RANK, MLA_HEADS, MLA_DN + MLA_DR)
    w_uq_t = jnp.concatenate(
        [w_uq3[:, :, :MLA_DN].reshape(MLA_Q_RANK, MLA_HEADS * MLA_DN),
         w_uq3[:, :, MLA_DN:].reshape(MLA_Q_RANK, MLA_HEADS * MLA_DR)], axis=1).astype(BF16).T
    w_ukv3 = w_ukv.reshape(MLA_KV_RANK, MLA_HEADS, MLA_DN + MLA_DV).astype(BF16)
    w_k = w_ukv3[:, :, :MLA_DN].reshape(MLA_KV_RANK, MLA_HEADS * MLA_DN)
    w_vt = w_ukv3[:, :, MLA_DN:].reshape(MLA_KV_RANK, MLA_HEADS * MLA_DV).T
    w_out_b = w_out.astype(BF16)
    inv_freq = ROPE_THETA ** (-jnp.arange(MLA_DR // 2, dtype=F32) / (MLA_DR // 2))
    invf_row = jnp.tile(inv_freq, LANES // (MLA_DR // 2)).reshape(1, LANES)
    invf_col = inv_freq.reshape(MLA_DR // 2, 1)

    mod = _adaln(c.reshape(d, 1), w_ada, b_ada.reshape(1, -1))
    h = _prenorm(x2, g_pre.reshape(1, d), mod)

    proj = _matmul(h, w_main, BF16, 1024, 1024, "inproj")
    xs = _matmul(h, w_small, F32, 1024, LANES, "inproj_small")

    o_a = _gla(proj, xs, wup, b_alpha.reshape(1, qk), g_gla_out.reshape(1, GLA_DV))

    cos, ss, cos_t, sin_t = _rope_tables(positions.reshape(s, 1), positions.reshape(1, s), invf_row, invf_col)
    qscale = (MLA_DN + MLA_DR) ** -0.5 * math.log2(math.e)
    qt = _qproj(proj, g_q_norm.reshape(1, MLA_Q_RANK), w_uq_t, cos_t, sin_t, qscale)
    kk, vt = _kvproj(proj, g_kv_norm.reshape(1, MLA_KV_RANK), w_k, w_vt, xs, cos, ss)
    o_b = _flash(qt, kk, vt, proj)

    mix = _outproj(o_a, o_b, w_out_b)
    out = _postnorm(mix, x2, g_post.reshape(1, d), mod)
    return out.reshape(b, s, d)
```
